```python
import jax, jax.numpy as jnp
from jax import lax
import numpy as np

D_MODEL = 1024
BATCH = 2
SEQ = 8192
DEPTH = 4
DEC_BATCH = 128
DEC_SEQ = 1
PAST_LEN = 2048
PAGE_SIZE = 128

D_CONV = D_MODEL
SCONV_W = 3
HG = 4
HD = 128
DILATED_GROUPS = ((128, 1), (512, 4), (2048, 16))
N_GROUPS = 3
D_ATT = HG * HD
D_RNN = (5 * D_MODEL) // 4
RG_CONV_W = 4
RG_BLOCKS = 10
RG_BLK = D_RNN // RG_BLOCKS
RG_C = 8.0
D_FF = 3 * D_MODEL
FFN_CONV_W = 3
N_MEM = 256
MEM_HEADS = 4
MEM_HD = D_MODEL // MEM_HEADS
ROPE_THETA = 10000.0
EPS = 1e-6
NEG = -1e30
OFF_B = 3 * D_CONV
OFF_C = OFF_B + N_GROUPS * 3 * D_ATT
OFF_G = OFF_C + 2 * D_RNN
IN_COLS = OFF_G + 3 * D_MODEL

kernel_name = 'hybrid_conv_dilated_rglru_decoder_step'


def rmsnorm(x, g):
    xf = x.astype(jnp.float32)
    y = xf * lax.rsqrt(jnp.mean(xf * xf, axis=-1, keepdims=True) + EPS)
    return (y * g.astype(jnp.float32)).astype(x.dtype)


def rope(x, pos):
    half = HD // 2
    inv = ROPE_THETA ** (-jnp.arange(half, dtype=jnp.float32) / half)
    ang = pos.astype(jnp.float32)[:, None] * inv[None, :]
    cos = jnp.cos(ang)[None, :, None, :]
    sin = jnp.sin(ang)[None, :, None, :]
    xf = x.astype(jnp.float32)
    x1, x2 = xf[..., :half], xf[..., half:]
    return jnp.concatenate([x1 * cos - x2 * sin, x2 * cos + x1 * sin], axis=-1).astype(x.dtype)


def causal_dwconv(u, prev, w):
    width = w.shape[0]
    t = u.shape[1]
    ext = jnp.concatenate([prev.astype(u.dtype), u], axis=1)
    y = ext[:, 0:t] * w[0]
    for k in range(1, width):
        y = y + ext[:, k:k + t] * w[k]
    return y, ext[:, t:]


def dilated_band_attention(q, k, v, dil, span):
    b, s, nh, hd = q.shape
    unit = dil * span
    padded = -(-s // unit) * unit
    pad = ((0, 0), (0, padded - s), (0, 0), (0, 0))
    n_cls = padded // dil
    nb = n_cls // span

    def blocks(t):
        t = jnp.pad(t, pad).reshape(b, n_cls, dil, nh, hd).transpose(0, 2, 1, 3, 4)
        return t.reshape(b, dil, nb, span, nh, hd)

    qb, kb, vb = blocks(q), blocks(k), blocks(v)
    prev = ((0, 0), (0, 0), (1, 0), (0, 0), (0, 0), (0, 0))
    kk = jnp.concatenate([jnp.pad(kb[:, :, :-1], prev), kb], axis=3)
    vv = jnp.concatenate([jnp.pad(vb[:, :, :-1], prev), vb], axis=3)
    scores = jnp.einsum('bcnqhd,bcnkhd->bcnhqk', qb, kk).astype(jnp.float32) * (hd ** -0.5)
    qi = jnp.arange(span)[:, None]
    kj = jnp.arange(2 * span)[None, :]
    dist = qi + span - kj
    band = (dist >= 0) & (dist <= span)
    has_prev = (jnp.arange(nb) > 0)[:, None, None] | (kj >= span)[None]
    mask = band[None] & has_prev
    scores = jnp.where(mask[:, None], scores, NEG)
    lse = jax.nn.logsumexp(scores, axis=-1)
    probs = jnp.exp(scores - lse[..., None]).astype(v.dtype)
    o = jnp.einsum('bcnhqk,bcnkhd->bcnqhd', probs, vv)
    o = o.reshape(b, dil, n_cls, nh, hd).transpose(0, 2, 1, 3, 4).reshape(b, padded, nh, hd)[:, :s]
    lse = lse.transpose(0, 1, 2, 4, 3).reshape(b, dil, n_cls, nh).transpose(0, 2, 1, 3).reshape(b, padded, nh)[:, :s]
    return o, lse


def dilated_gather_attention(q, k, v, buf, dil, span):
    b, t, nh, hd = q.shape
    wb = buf.shape[1]
    ke = jnp.concatenate([buf[:, :, 0].astype(k.dtype), k], axis=1)
    ve = jnp.concatenate([buf[:, :, 1].astype(v.dtype), v], axis=1)
    idx = wb + jnp.arange(t)[:, None] - dil * jnp.arange(span + 1)[None, :]
    valid = idx >= 0
    idx = jnp.maximum(idx, 0)
    kg = ke[:, idx]
    vg = ve[:, idx]
    scores = jnp.einsum('bthd,btkhd->bthk', q, kg).astype(jnp.float32) * (hd ** -0.5)
    scores = jnp.where(valid[None, :, None, :], scores, NEG)
    lse = jax.nn.logsumexp(scores, axis=-1)
    probs = jnp.exp(scores - lse[..., None]).astype(v.dtype)
    o = jnp.einsum('bthk,btkhd->bthd', probs, vg)
    return o, lse


def rglru(xr, h0, w_a, b_a, w_i, b_i, lam):
    b, t, _ = xr.shape
    xb = xr.reshape(b, t, RG_BLOCKS, RG_BLK)
    r = jax.nn.sigmoid(jnp.einsum('btnc,ncd->btnd', xb, w_a).reshape(b, t, D_RNN) + b_a)
    gi = jax.nn.sigmoid(jnp.einsum('btnc,ncd->btnd', xb, w_i).reshape(b, t, D_RNN) + b_i)
    log_a = (-RG_C * r.astype(jnp.float32)) * jax.nn.softplus(-lam.astype(jnp.float32))
    a = jnp.exp(log_a)
    u = jnp.sqrt(-jnp.expm1(2.0 * log_a)) * (gi * xr).astype(jnp.float32)
    u = u.at[:, 0].add(a[:, 0] * h0.astype(jnp.float32))

    def comb(left, right):
        a1, b1 = left
        a2, b2 = right
        return a1 * a2, a2 * b1 + b2

    _, h = lax.associative_scan(comb, (a, u), axis=1)
    return h.astype(xr.dtype), h[:, -1].astype(xr.dtype)


def mixer_block(x, pos, p, l, swa_prev, sconv_prev, rconv_prev, h_prev, prompt):
    b, t, _ = x.shape
    h = rmsnorm(x, p['norm_mix'][l])
    z = h @ p['w_in'][l]
    z_a, z_b, z_c, z_g = jnp.split(z, [OFF_B, OFF_C, OFF_G], axis=-1)
    b_a, c_a, x_a = jnp.split(z_a, 3, axis=-1)
    conv_a, new_sconv = causal_dwconv(c_a * x_a, sconv_prev, p['w_sconv'][l])
    y_a = (b_a * conv_a) @ p['w_a_out'][l]
    z_b = z_b.reshape(b, t, N_GROUPS, 3, HG, HD)
    outs, lses, new_swa = [], [], []
    for g, (win, dil) in enumerate(DILATED_GROUPS):
        q = rope(z_b[:, :, g, 0], pos)
        k = rope(z_b[:, :, g, 1], pos)
        v = z_b[:, :, g, 2]
        kv = jnp.stack([k, v], axis=2)
        if prompt:
            o, lse = dilated_band_attention(q, k, v, dil, win // dil)
            new_swa.append(kv[:, -win:])
        else:
            o, lse = dilated_gather_attention(q, k, v, swa_prev[g], dil, win // dil)
            new_swa.append(kv)
        outs.append(o)
        lses.append(lse)
    wts = jax.nn.softmax(jnp.stack(lses, axis=0), axis=0).astype(x.dtype)
    o = wts[0][..., None] * outs[0]
    for g in range(1, N_GROUPS):
        o = o + wts[g][..., None] * outs[g]
    y_b = o.reshape(b, t, D_ATT) @ p['w_b_out'][l]
    x_c, gate_c = jnp.split(z_c, 2, axis=-1)
    conv_c, new_rconv = causal_dwconv(x_c, rconv_prev, p['w_rconv'][l])
    conv_c = conv_c + p['b_rconv'][l]
    h_c, new_h = rglru(conv_c, h_prev, p['rg_w_a'][l], p['rg_b_a'][l], p['rg_w_i'][l], p['rg_b_i'][l], p['rg_lambda'][l])
    y_c = (h_c * jax.nn.gelu(gate_c)) @ p['w_c_out'][l]
    g_a, g_b, g_c = jnp.split(jax.nn.sigmoid(z_g), 3, axis=-1)
    x = x + (g_a * y_a + g_b * y_b + g_c * y_c) @ p['w_mix_out'][l]
    return x, new_swa, new_sconv, new_rconv, new_h


def memory_kv(mem, p, l):
    m = rmsnorm(mem, p['norm_memkv'][l])
    return (m @ p['w_mkv'][l]).reshape(mem.shape[0], mem.shape[1], 2, MEM_HEADS, MEM_HD)


def memory_attention(x, mem_kv, p, l):
    b, t, _ = x.shape
    h = rmsnorm(x, p['norm_mem'][l])
    q = (h @ p['w_mq'][l]).reshape(b, t, MEM_HEADS, MEM_HD)
    s = jnp.einsum('bthd,bmhd->bhtm', q, mem_kv[:, :, 0].astype(x.dtype)).astype(jnp.float32) * (MEM_HD ** -0.5)
    pr = jax.nn.softmax(s, axis=-1).astype(x.dtype)
    o = jnp.einsum('bhtm,bmhd->bthd', pr, mem_kv[:, :, 1].astype(x.dtype))
    return x + o.reshape(b, t, D_MODEL) @ p['w_mo'][l]


def conv_ffn(x, prev, p, l):
    h = rmsnorm(x, p['norm_ffn'][l])
    u = h @ p['w_up'][l]
    gt = h @ p['w_gate'][l]
    uc, new_prev = causal_dwconv(u, prev, p['w_fconv'][l])
    return x + (jax.nn.gelu(uc) * gt) @ p['w_down'][l], new_prev


def setup_inputs(seed: int = 0) -> dict:
    key = jax.random.key(seed)
    ks = iter(jax.random.split(key, 48))

    def nrm(shape, scale):
        return jax.random.normal(next(ks), shape, jnp.float32) * scale

    def gain():
        return 1.0 + nrm((DEPTH, D_MODEL), 0.02)

    wb = [min(w, PAST_LEN) for w, _ in DILATED_GROUPS]
    u = jax.random.uniform(next(ks), (DEPTH, D_RNN), jnp.float32, minval=0.9, maxval=0.999)
    s = u ** (1.0 / RG_C)
    rg_lambda = jnp.log(s) - jnp.log1p(-s)
    return {
        'x_prompt': nrm((BATCH, SEQ, D_MODEL), 1.0),
        'x_sample': nrm((DEC_BATCH, DEC_SEQ, D_MODEL), 1.0),
        'cache_swa1_kv': nrm((DEPTH, DEC_BATCH, wb[0], 2, HG, HD), 1.0),
        'cache_swa2_kv': nrm((DEPTH, DEC_BATCH, wb[1], 2, HG, HD), 1.0),
        'cache_swa3_kv': nrm((DEPTH, DEC_BATCH, wb[2], 2, HG, HD), 1.0),
        'cache_mem_kv': nrm((DEPTH, DEC_BATCH, N_MEM, 2, MEM_HEADS, MEM_HD), 1.0),
        'state_sconv': nrm((DEPTH, DEC_BATCH, SCONV_W - 1, D_CONV), 1.0),
        'state_rglru_conv': nrm((DEPTH, DEC_BATCH, RG_CONV_W - 1, D_RNN), 1.0),
        'state_rglru_h': nrm((DEPTH, DEC_BATCH, D_RNN), 0.5),
        'state_ffn_conv': nrm((DEPTH, DEC_BATCH, FFN_CONV_W - 1, D_FF), 1.0),
        'mem_prompt': nrm((BATCH, N_MEM, D_MODEL), 1.0),
        'norm_mix': gain(),
        'w_in': nrm((DEPTH, D_MODEL, IN_COLS), D_MODEL ** -0.5),
        'w_sconv': nrm((DEPTH, SCONV_W, D_CONV), SCONV_W ** -0.5),
        'w_a_out': nrm((DEPTH, D_CONV, D_MODEL), D_CONV ** -0.5),
        'w_b_out': nrm((DEPTH, D_ATT, D_MODEL), D_ATT ** -0.5),
        'w_rconv': nrm((DEPTH, RG_CONV_W, D_RNN), RG_CONV_W ** -0.5),
        'b_rconv': nrm((DEPTH, D_RNN), 0.01),
        'rg_w_a': nrm((DEPTH, RG_BLOCKS, RG_BLK, RG_BLK), RG_BLK ** -0.5),
        'rg_b_a': nrm((DEPTH, D_RNN), 0.01),
        'rg_w_i': nrm((DEPTH, RG_BLOCKS, RG_BLK, RG_BLK), RG_BLK ** -0.5),
        'rg_b_i': nrm((DEPTH, D_RNN), 0.01),
        'rg_lambda': rg_lambda,
        'w_c_out': nrm((DEPTH, D_RNN, D_MODEL), D_RNN ** -0.5),
        'w_mix_out': nrm((DEPTH, D_MODEL, D_MODEL), D_MODEL ** -0.5),
        'norm_mem': gain(),
        'norm_memkv': gain(),
        'w_mq': nrm((DEPTH, D_MODEL, D_MODEL), D_MODEL ** -0.5),
        'w_mkv': nrm((DEPTH, D_MODEL, 2 * D_MODEL), D_MODEL ** -0.5),
        'w_mo': nrm((DEPTH, D_MODEL, D_MODEL), D_MODEL ** -0.5),
        'norm_ffn': gain(),
        'w_up': nrm((DEPTH, D_MODEL, D_FF), D_MODEL ** -0.5),
        'w_gate': nrm((DEPTH, D_MODEL, D_FF), D_MODEL ** -0.5),
        'w_fconv': nrm((DEPTH, FFN_CONV_W, D_FF), FFN_CONV_W ** -0.5),
        'w_down': nrm((DEPTH, D_FF, D_MODEL), D_FF ** -0.5),
        'norm_final': 1.0 + nrm((D_MODEL,), 0.02),
    }


def reference(x_prompt, x_sample, cache_swa1_kv, cache_swa2_kv, cache_swa3_kv, cache_mem_kv,
              state_sconv, state_rglru_conv, state_rglru_h, state_ffn_conv, mem_prompt,
              norm_mix, w_in, w_sconv, w_a_out, w_b_out, w_rconv, b_rconv, rg_w_a, rg_b_a,
              rg_w_i, rg_b_i, rg_lambda, w_c_out, w_mix_out, norm_mem, norm_memkv, w_mq, w_mkv,
              w_mo, norm_ffn, w_up, w_gate, w_fconv, w_down, norm_final):
    p = {'norm_mix': norm_mix, 'w_in': w_in, 'w_sconv': w_sconv, 'w_a_out': w_a_out,
         'w_b_out': w_b_out, 'w_rconv': w_rconv, 'b_rconv': b_rconv, 'rg_w_a': rg_w_a,
         'rg_b_a': rg_b_a, 'rg_w_i': rg_w_i, 'rg_b_i': rg_b_i, 'rg_lambda': rg_lambda,
         'w_c_out': w_c_out, 'w_mix_out': w_mix_out, 'norm_mem': norm_mem,
         'norm_memkv': norm_memkv, 'w_mq': w_mq, 'w_mkv': w_mkv, 'w_mo': w_mo,
         'norm_ffn': norm_ffn, 'w_up': w_up, 'w_gate': w_gate, 'w_fconv': w_fconv,
         'w_down': w_down}
    bp, sp, _ = x_prompt.shape
    ts = x_sample.shape[1]
    dt = x_prompt.dtype
    pos_p = jnp.arange(sp)
    pos_s = PAST_LEN + jnp.arange(ts)
    zero_sconv = jnp.zeros((bp, SCONV_W - 1, D_CONV), dt)
    zero_rconv = jnp.zeros((bp, RG_CONV_W - 1, D_RNN), dt)
    zero_h = jnp.zeros((bp, D_RNN), dt)
    zero_fconv = jnp.zeros((bp, FFN_CONV_W - 1, D_FF), dt)
    names = ('p_swa1', 'p_swa2', 'p_swa3', 'p_mem', 'p_sconv', 'p_rconv', 'p_h', 'p_fconv',
             's_swa1', 's_swa2', 's_swa3', 's_sconv', 's_rconv', 's_h', 's_fconv')
    acc = {n: [] for n in names}
    xp, xs = x_prompt, x_sample
    for l in range(DEPTH):
        mkv_p = memory_kv(mem_prompt, p, l)
        xp, sw_p, sc_p, rc_p, h_p = mixer_block(xp, pos_p, p, l, None, zero_sconv, zero_rconv, zero_h, True)
        xp = memory_attention(xp, mkv_p, p, l)
        xp, fc_p = conv_ffn(xp, zero_fconv, p, l)
        swa_prev = (cache_swa1_kv[l], cache_swa2_kv[l], cache_swa3_kv[l])
        xs, sw_s, sc_s, rc_s, h_s = mixer_block(xs, pos_s, p, l, swa_prev, state_sconv[l],
                                                state_rglru_conv[l], state_rglru_h[l], False)
        xs = memory_attention(xs, cache_mem_kv[l], p, l)
        xs, fc_s = conv_ffn(xs, state_ffn_conv[l], p, l)
        acc['p_swa1'].append(sw_p[0])
        acc['p_swa2'].append(sw_p[1])
        acc['p_swa3'].append(sw_p[2])
        acc['p_mem'].append(mkv_p)
        acc['p_sconv'].append(sc_p)
        acc['p_rconv'].append(rc_p)
        acc['p_h'].append(h_p)
        acc['p_fconv'].append(fc_p)
        acc['s_swa1'].append(sw_s[0])
        acc['s_swa2'].append(sw_s[1])
        acc['s_swa3'].append(sw_s[2])
        acc['s_sconv'].append(sc_s)
        acc['s_rconv'].append(rc_s)
        acc['s_h'].append(h_s)
        acc['s_fconv'].append(fc_s)
    y_prompt = rmsnorm(xp, norm_final)
    y_sample = rmsnorm(xs, norm_final)
    return (y_prompt, y_sample,
            jnp.stack(acc['p_swa1']), jnp.stack(acc['p_swa2']), jnp.stack(acc['p_swa3']),
            jnp.stack(acc['p_mem']), jnp.stack(acc['p_sconv']), jnp.stack(acc['p_rconv']),
            jnp.stack(acc['p_h']), jnp.stack(acc['p_fconv']),
            jnp.stack(acc['s_swa1']), jnp.stack(acc['s_swa2']), jnp.stack(acc['s_swa3']),
            jnp.stack(acc['s_sconv']), jnp.stack(acc['s_rconv']), jnp.stack(acc['s_h']),
            jnp.stack(acc['s_fconv']))
```

```python
import functools

import jax
import jax.numpy as jnp
from jax import lax
from jax.experimental import pallas as pl
from jax.experimental.pallas import tpu as pltpu

F32 = jnp.float32
BF16 = jnp.bfloat16

D_MODEL = 1024
DEPTH = 4
PAST_LEN = 2048
D_CONV = D_MODEL
HG = 4
HD = 128
DILATED_GROUPS = ((128, 1), (512, 4), (2048, 16))
N_GROUPS = 3
SPAN = 128
D_ATT = HG * HD
D_RNN = (5 * D_MODEL) // 4
RG_BLOCKS = 10
RG_BLK = D_RNN // RG_BLOCKS
RG_C = 8.0
D_FF = 3 * D_MODEL
N_MEM = 256
MEM_HEADS = 4
MEM_HD = D_MODEL // MEM_HEADS
ROPE_THETA = 10000.0
EPS = 1e-6
NEG = -1e30
OFF_B = 3 * D_CONV
OFF_C = OFF_B + N_GROUPS * 3 * D_ATT
OFF_G = OFF_C + 2 * D_RNN
IN_COLS = OFF_G + 3 * D_MODEL

SUBLANES = 8
VMEM_LIMIT_BYTES = 56 * 1024 * 1024
NT_DIMS = (((1,), (1,)), ((), ()))


def _params(n_axes):
    return pltpu.CompilerParams(dimension_semantics=("arbitrary",) * n_axes,
                                vmem_limit_bytes=VMEM_LIMIT_BYTES)


def _resident(block_shape, index_map):
    return pl.BlockSpec(block_shape, index_map, pipeline_mode=pl.Buffered(1))


def _rms(x, g):
    return x * lax.rsqrt(jnp.mean(x * x, axis=-1, keepdims=True) + EPS) * g


def _gelu(x):
    return x * (0.5 * (1.0 + jnp.tanh(0.7978845608028654 * (x + 0.044715 * (x * x * x)))))


def _softplus(x):
    return jnp.maximum(x, 0.0) + jnp.log1p(jnp.exp(-jnp.abs(x)))


def _rope_head(x, cosf, sinf):
    return x * cosf + pltpu.roll(x, HD // 2, 1) * sinf


def _rms_matmul_body(x_ref, g_ref, w_ref, o_ref, hn_ref):
    @pl.when(pl.program_id(1) == 0)
    def _():
        hn_ref[...] = _rms(x_ref[...], g_ref[...]).astype(BF16)

    o_ref[...] = jnp.dot(hn_ref[...], w_ref[...], preferred_element_type=F32)


def rms_matmul(x, g, w, layer, tm, tn):
    n, k = x.shape
    cols = w.shape[2]
    return pl.pallas_call(
        _rms_matmul_body,
        grid=(n // tm, cols // tn),
        in_specs=[pl.BlockSpec((tm, k), lambda i, j: (i, 0)),
                  pl.BlockSpec((None, 1, k), lambda i, j: (layer, 0, 0)),
                  pl.BlockSpec((None, k, tn), lambda i, j: (layer, 0, j))],
        out_specs=pl.BlockSpec((tm, tn), lambda i, j: (i, j)),
        out_shape=jax.ShapeDtypeStruct((n, cols), F32),
        scratch_shapes=[pltpu.VMEM((tm, k), BF16)],
        compiler_params=_params(2),
        name="rms_matmul",
    )(x, g, w)


def _rmsnorm_body(x_ref, g_ref, o_ref):
    o_ref[...] = _rms(x_ref[...], g_ref[...])


def rmsnorm_rows(x, g, tm):
    n, k = x.shape
    return pl.pallas_call(
        _rmsnorm_body,
        grid=(n // tm,),
        in_specs=[pl.BlockSpec((tm, k), lambda i: (i, 0)),
                  pl.BlockSpec((1, k), lambda i: (0, 0))],
        out_specs=pl.BlockSpec((tm, k), lambda i: (i, 0)),
        out_shape=jax.ShapeDtypeStruct((n, k), F32),
        compiler_params=_params(1),
        name="final_rmsnorm",
    )(x, g)


def _proj_residual_body(x_ref, a_ref, w_ref, o_ref):
    o_ref[...] = x_ref[...] + jnp.dot(a_ref[...].astype(BF16), w_ref[...],
                                      preferred_element_type=F32)


def proj_residual(x, a, w, layer):
    n, k = a.shape
    cols = w.shape[2]
    return pl.pallas_call(
        _proj_residual_body,
        grid=(1,),
        in_specs=[pl.BlockSpec((n, cols), lambda i: (0, 0)),
                  pl.BlockSpec((n, k), lambda i: (0, 0)),
                  pl.BlockSpec((None, k, cols), lambda i: (layer, 0, 0))],
        out_specs=pl.BlockSpec((n, cols), lambda i: (0, 0)),
        out_shape=jax.ShapeDtypeStruct((n, cols), F32),
        compiler_params=_params(1),
        name="proj_residual",
    )(x, a, w)


def _rope_class_major_body(q_ref, k_ref, v_ref, t_ref, qo_ref, ko_ref, vo_ref):
    t = t_ref[...]
    cosf, sinf = t[:, :HD], t[:, HD:]
    for h in range(HG):
        hs = slice(h * HD, (h + 1) * HD)
        qo_ref[:, hs] = _rope_head(q_ref[:, hs], cosf, sinf).astype(BF16)
        ko_ref[:, hs] = _rope_head(k_ref[:, hs], cosf, sinf).astype(BF16)
    vo_ref[...] = v_ref[...].astype(BF16)


def rope_class_major(z, tbl, g, dil, bsz, s):
    sd = s // dil
    tn = min(sd, 1024)
    nbk = sd // tn
    zv = z.reshape(bsz, sd, dil * IN_COLS)
    tv = tbl.reshape(sd, dil * 2 * HD)
    cb = IN_COLS // D_ATT
    base = (OFF_B + g * 3 * D_ATT) // D_ATT

    def zspec(off):
        return pl.BlockSpec((None, tn, D_ATT), lambda b, c, i: (b, i, c * cb + base + off))

    ospec = pl.BlockSpec((None, tn, D_ATT), lambda b, c, i: (b, c * nbk + i, 0))
    oshape = jax.ShapeDtypeStruct((bsz, s, D_ATT), BF16)
    return pl.pallas_call(
        _rope_class_major_body,
        grid=(bsz, dil, nbk),
        in_specs=[zspec(0), zspec(1), zspec(2),
                  pl.BlockSpec((tn, 2 * HD), lambda b, c, i: (i, c))],
        out_specs=[ospec, ospec, ospec],
        out_shape=[oshape, oshape, oshape],
        compiler_params=_params(3),
        name=f"rope_class_major_g{g}",
    )(zv, zv, zv, tv)


def _rope_tail_body(k_ref, v_ref, t_ref, o_ref):
    t = t_ref[...]
    cosf, sinf = t[:, :HD], t[:, HD:]
    for h in range(HG):
        hs = slice(h * HD, (h + 1) * HD)
        o_ref[:, hs] = _rope_head(k_ref[:, hs], cosf, sinf)
    o_ref[:, D_ATT:] = v_ref[...]


def rope_tail(z, tbl, bsz, s, tail):
    tr = min(tail, 512)
    nt = tail // tr
    row0 = (s - tail) // tr
    per_seq = s // tr

    def zspec(off):
        return pl.BlockSpec(
            (tr, D_ATT),
            lambda b, g, i: (b * per_seq + row0 + i, (OFF_B + g * 3 * D_ATT) // D_ATT + off))

    return pl.pallas_call(
        _rope_tail_body,
        grid=(bsz, N_GROUPS, nt),
        in_specs=[zspec(1), zspec(2),
                  pl.BlockSpec((tr, 2 * HD), lambda b, g, i: (row0 + i, 0))],
        out_specs=pl.BlockSpec((tr, 2 * D_ATT), lambda b, g, i: (b * nt + i, g)),
        out_shape=jax.ShapeDtypeStruct((bsz * tail, N_GROUPS * 2 * D_ATT), F32),
        compiler_params=_params(3),
        name="rope_tail",
    )(z, z, tbl)


def _band_attn_body(q_ref, kp_ref, kc_ref, vp_ref, vc_ref, o_ref, lse_ref, *, nsub, blocks_per_class):
    i = pl.program_id(1)
    qi = lax.broadcasted_iota(jnp.int32, (SPAN, 2 * SPAN), 0)
    kj = lax.broadcasted_iota(jnp.int32, (SPAN, 2 * SPAN), 1)
    dist = qi + SPAN - kj
    band = (dist >= 0) & (dist <= SPAN)
    lane = lax.broadcasted_iota(jnp.int32, (SPAN, HD), 1)
    scale = HD ** -0.5
    for s in range(nsub):
        rows = slice(s * SPAN, (s + 1) * SPAN)
        first = ((i * nsub + s) % blocks_per_class) == 0
        kmin = jnp.where(first, SPAN, 0)
        mask = band & (kj >= kmin)
        lse_tile = jnp.zeros((SPAN, HD), F32)
        for h in range(HG):
            hs = slice(h * HD, (h + 1) * HD)
            q = q_ref[rows, hs]
            if s == 0:
                k = jnp.concatenate([kp_ref[:, hs], kc_ref[0:SPAN, hs]], axis=0)
                v = jnp.concatenate([vp_ref[:, hs], vc_ref[0:SPAN, hs]], axis=0)
            else:
                k = kc_ref[(s - 1) * SPAN:(s + 1) * SPAN, hs]
                v = vc_ref[(s - 1) * SPAN:(s + 1) * SPAN, hs]
            sc = lax.dot_general(q, k, NT_DIMS, preferred_element_type=F32) * scale
            sc = jnp.where(mask, sc, NEG)
            m = jnp.max(sc, axis=-1, keepdims=True)
            e = jnp.exp(sc - m)
            l = jnp.sum(e, axis=-1, keepdims=True)
            p = (e / l).astype(BF16)
            o_ref[rows, hs] = jnp.dot(p, v, preferred_element_type=F32)
            lse_tile = jnp.where(lane == h, m + jnp.log(l), lse_tile)
        lse_ref[rows, :] = lse_tile


def band_attention(qc, kc, vc, dil, bsz, s):
    sd = s // dil
    qb = min(sd, 512)
    nsub = qb // SPAN
    npc = sd // qb
    cur = pl.BlockSpec((None, qb, D_ATT), lambda b, i: (b, i, 0))
    prev = pl.BlockSpec((None, SPAN, D_ATT), lambda b, i: (b, jnp.maximum(i * nsub - 1, 0), 0))
    o, lse = pl.pallas_call(
        functools.partial(_band_attn_body, nsub=nsub, blocks_per_class=sd // SPAN),
        grid=(bsz, s // qb),
        in_specs=[cur, prev, cur, prev, cur],
        out_specs=[pl.BlockSpec((None, qb, D_ATT), lambda b, i: (b, i % npc, i // npc)),
                   pl.BlockSpec((None, qb, HD), lambda b, i: (b, i % npc, i // npc))],
        out_shape=[jax.ShapeDtypeStruct((bsz, sd, dil * D_ATT), F32),
                   jax.ShapeDtypeStruct((bsz, sd, dil * HD), F32)],
        compiler_params=_params(2),
        name=f"band_attention_d{dil}",
    )(qc, kc, kc, vc, vc)
    return o.reshape(bsz * s, D_ATT), lse.reshape(bsz * s, HD)


def _gather_attn_body(q_ref, k_ref, v_ref, t_ref, c_ref, o_ref, lse_ref, kv_ref, *, tb):
    t = t_ref[...]
    cosf, sinf = t[:, :HD], t[:, HD:]
    lane = lax.broadcasted_iota(jnp.int32, (tb, HD), 1)
    scale = HD ** -0.5
    lse_tile = jnp.zeros((tb, HD), F32)
    for h in range(HG):
        hs = slice(h * HD, (h + 1) * HD)
        vs = slice(D_ATT + h * HD, D_ATT + (h + 1) * HD)
        q = _rope_head(q_ref[:, hs], cosf, sinf)
        kn = _rope_head(k_ref[:, hs], cosf, sinf)
        vn = v_ref[:, hs]
        kv_ref[:, hs] = kn
        kv_ref[:, vs] = vn
        s_self = jnp.sum(q * kn, axis=-1, keepdims=True) * scale
        lse_col = []
        for b in range(tb):
            qb = q[b:b + 1, :]
            sc = jnp.sum(c_ref[b, :, hs] * qb, axis=-1, keepdims=True) * scale
            ss = s_self[b:b + 1, :]
            m = jnp.maximum(jnp.max(sc, axis=0, keepdims=True), ss)
            e = jnp.exp(sc - m)
            es = jnp.exp(ss - m)
            l = jnp.sum(e, axis=0, keepdims=True) + es
            acc = jnp.sum(e * c_ref[b, :, vs], axis=0, keepdims=True) + es * vn[b:b + 1, :]
            o_ref[b:b + 1, hs] = acc / l
            lse_col.append(m + jnp.log(l))
        lse_h = jnp.concatenate(lse_col, axis=0)
        lse_tile = jnp.where(lane == h, lse_h, lse_tile)
    lse_ref[...] = lse_tile


def gather_attention(zs, tbl_s, cache, layer, g, dil):
    m = zs.shape[0]
    w = cache.shape[2]
    assert w == dil * SPAN, "window buffer must hold exactly span*dil past rows"
    tb = SUBLANES
    cv = cache.reshape(cache.shape[0], m, SPAN, dil * 2 * D_ATT)
    base = (OFF_B + g * 3 * D_ATT) // D_ATT

    def zspec(off):
        return pl.BlockSpec((tb, D_ATT), lambda i: (i, base + off))

    return pl.pallas_call(
        functools.partial(_gather_attn_body, tb=tb),
        grid=(m // tb,),
        in_specs=[zspec(0), zspec(1), zspec(2),
                  pl.BlockSpec((1, 2 * HD), lambda i: (0, 0)),
                  pl.BlockSpec((None, tb, SPAN, 2 * D_ATT), lambda i: (layer, i, 0, 0))],
        out_specs=[pl.BlockSpec((tb, D_ATT), lambda i: (i, 0)),
                   pl.BlockSpec((tb, HD), lambda i: (i, 0)),
                   pl.BlockSpec((tb, 2 * D_ATT), lambda i: (i, 0))],
        out_shape=[jax.ShapeDtypeStruct((m, D_ATT), F32),
                   jax.ShapeDtypeStruct((m, HD), F32),
                   jax.ShapeDtypeStruct((m, 2 * D_ATT), F32)],
        compiler_params=_params(1),
        name=f"gather_attention_g{g}",
    )(zs, zs, zs, tbl_s, cv)


def _combine_groups(o_refs, l_refs):
    ls = [r[...] for r in l_refs]
    m = jnp.maximum(jnp.maximum(ls[0], ls[1]), ls[2])
    es = [jnp.exp(l - m) for l in ls]
    den = es[0] + es[1] + es[2]
    ws = [e / den for e in es]
    outs = []
    for h in range(HG):
        hs = slice(h * HD, (h + 1) * HD)
        acc = ws[0][:, h:h + 1] * o_refs[0][:, hs]
        for g in range(1, N_GROUPS):
            acc = acc + ws[g][:, h:h + 1] * o_refs[g][:, hs]
        outs.append(acc)
    return jnp.concatenate(outs, axis=1)


def _rglru_gates(conv_c, rwa_ref, rba_ref, rwi_ref, rbi_ref, lam_ref):
    cb = conv_c.astype(BF16)
    rs, gis = [], []
    for n in range(RG_BLOCKS):
        blk = cb[:, n * RG_BLK:(n + 1) * RG_BLK]
        rs.append(jnp.dot(blk, rwa_ref[n], preferred_element_type=F32))
        gis.append(jnp.dot(blk, rwi_ref[n], preferred_element_type=F32))
    r = jax.nn.sigmoid(jnp.concatenate(rs, axis=1) + rba_ref[...])
    gi = jax.nn.sigmoid(jnp.concatenate(gis, axis=1) + rbi_ref[...])
    log_a = (-RG_C * r) * _softplus(-lam_ref[...])
    a = jnp.exp(log_a)
    u = jnp.sqrt(1.0 - jnp.exp(2.0 * log_a)) * (gi * conv_c)
    return a, u


def _linear_scan(a, u, tm):
    row = lax.broadcasted_iota(jnp.int32, a.shape, 0)
    k = 1
    while k < tm:
        keep = row >= k
        u = jnp.where(keep, a * pltpu.roll(u, k, 0) + u, u)
        a = jnp.where(keep, a * pltpu.roll(a, k, 0), a)
        k *= 2
    return a, u


def _mixer_merge(x_ref, ba_ref, conv_a, o_refs, l_refs, conv_c, gc_ref, g_refs, w, h_fn, xo_ref):
    (rwa_ref, rba_ref, rwi_ref, rbi_ref, lam_ref, waout_ref, wbout_ref, wcout_ref, wmix_ref) = w
    ya = jnp.dot((ba_ref[...] * conv_a).astype(BF16), waout_ref[...], preferred_element_type=F32)
    mix = jax.nn.sigmoid(g_refs[0][...]) * ya
    ob = _combine_groups(o_refs, l_refs)
    yb = jnp.dot(ob.astype(BF16), wbout_ref[...], preferred_element_type=F32)
    mix = mix + jax.nn.sigmoid(g_refs[1][...]) * yb
    a, u = _rglru_gates(conv_c, rwa_ref, rba_ref, rwi_ref, rbi_ref, lam_ref)
    h = h_fn(a, u)
    yc = jnp.dot((h * _gelu(gc_ref[...])).astype(BF16), wcout_ref[...], preferred_element_type=F32)
    mix = mix + jax.nn.sigmoid(g_refs[2][...]) * yc
    xo_ref[...] = x_ref[...] + jnp.dot(mix.astype(BF16), wmix_ref[...], preferred_element_type=F32)
    return h


def _mixer_prompt_body(x_ref, ba_ref, ca_ref, xa_ref, xc_ref, gc_ref, g0_ref, g1_ref, g2_ref,
                       o0_ref, o1_ref, o2_ref, l0_ref, l1_ref, l2_ref,
                       wsc_ref, wrc_ref, brc_ref, rwa_ref, rba_ref, rwi_ref, rbi_ref, lam_ref,
                       waout_ref, wbout_ref, wcout_ref, wmix_ref,
                       xo_ref, sc_ref, h_ref,
                       pa_s, xc_s, hc_s, *, tm):
    hdr = SUBLANES

    @pl.when(pl.program_id(1) == 0)
    def _():
        pa_s[0:hdr, :] = jnp.zeros((hdr, D_CONV), F32)
        xc_s[0:hdr, :] = jnp.zeros((hdr, D_RNN), F32)
        hc_s[...] = jnp.zeros((hdr, D_RNN), F32)

    pa = ca_ref[...] * xa_ref[...]
    pa_s[hdr:hdr + tm, :] = pa
    wsc = wsc_ref[...]
    conv_a = (pa_s[hdr - 2:hdr - 2 + tm, :] * wsc[0:1, :]
              + pa_s[hdr - 1:hdr - 1 + tm, :] * wsc[1:2, :]
              + pa * wsc[2:3, :])
    pa_s[0:hdr, :] = pa[tm - hdr:tm, :]
    sc_ref[...] = pa[tm - hdr:tm, :]

    xc = xc_ref[...]
    xc_s[hdr:hdr + tm, :] = xc
    wrc = wrc_ref[...]
    conv_c = (xc_s[hdr - 3:hdr - 3 + tm, :] * wrc[0:1, :]
              + xc_s[hdr - 2:hdr - 2 + tm, :] * wrc[1:2, :]
              + xc_s[hdr - 1:hdr - 1 + tm, :] * wrc[2:3, :]
              + xc * wrc[3:4, :]) + brc_ref[...]
    xc_s[0:hdr, :] = xc[tm - hdr:tm, :]

    def h_fn(a, u):
        ac, uc = _linear_scan(a, u, tm)
        h = ac * hc_s[0:1, :] + uc
        hc_s[...] = jnp.broadcast_to(h[tm - 1:tm, :], (hdr, D_RNN))
        return h

    w = (rwa_ref, rba_ref, rwi_ref, rbi_ref, lam_ref, waout_ref, wbout_ref, wcout_ref, wmix_ref)
    h = _mixer_merge(x_ref, ba_ref, conv_a, (o0_ref, o1_ref, o2_ref), (l0_ref, l1_ref, l2_ref),
                     conv_c, gc_ref, (g0_ref, g1_ref, g2_ref), w, h_fn, xo_ref)
    h_ref[...] = h[tm - hdr:tm, :]


def _mixer_sample_body(x_ref, ba_ref, ca_ref, xa_ref, xc_ref, gc_ref, g0_ref, g1_ref, g2_ref,
                       o0_ref, o1_ref, o2_ref, l0_ref, l1_ref, l2_ref,
                       ssc_ref, src_ref, sh_ref,
                       wsc_ref, wrc_ref, brc_ref, rwa_ref, rba_ref, rwi_ref, rbi_ref, lam_ref,
                       waout_ref, wbout_ref, wcout_ref, wmix_ref,
                       xo_ref, sco_ref, rco_ref, ho_ref):
    pa = ca_ref[...] * xa_ref[...]
    wsc = wsc_ref[...]
    conv_a = (ssc_ref[:, 0:D_CONV] * wsc[0:1, :] + ssc_ref[:, D_CONV:] * wsc[1:2, :]
              + pa * wsc[2:3, :])
    sco_ref[:, 0:D_CONV] = ssc_ref[:, D_CONV:]
    sco_ref[:, D_CONV:] = pa

    xc = xc_ref[...]
    wrc = wrc_ref[...]
    conv_c = (src_ref[:, 0:D_RNN] * wrc[0:1, :] + src_ref[:, D_RNN:2 * D_RNN] * wrc[1:2, :]
              + src_ref[:, 2 * D_RNN:] * wrc[2:3, :] + xc * wrc[3:4, :]) + brc_ref[...]
    rco_ref[:, 0:2 * D_RNN] = src_ref[:, D_RNN:]
    rco_ref[:, 2 * D_RNN:] = xc

    def h_fn(a, u):
        return u + a * sh_ref[...]

    w = (rwa_ref, rba_ref, rwi_ref, rbi_ref, lam_ref, waout_ref, wbout_ref, wcout_ref, wmix_ref)
    h = _mixer_merge(x_ref, ba_ref, conv_a, (o0_ref, o1_ref, o2_ref), (l0_ref, l1_ref, l2_ref),
                     conv_c, gc_ref, (g0_ref, g1_ref, g2_ref), w, h_fn, xo_ref)
    ho_ref[...] = h


def _mixer_weight_specs(layer, nidx):
    def const(shape):
        zeros = (0,) * (len(shape) - 1)
        if nidx == 2:
            return _resident(shape, lambda b, j: (layer,) + zeros)
        return _resident(shape, lambda i: (layer,) + zeros)

    return [const((None, 3, D_CONV)), const((None, 4, D_RNN)), const((None, 1, D_RNN)),
            const((None, RG_BLOCKS, RG_BLK, RG_BLK)), const((None, 1, D_RNN)),
            const((None, RG_BLOCKS, RG_BLK, RG_BLK)), const((None, 1, D_RNN)),
            const((None, 1, D_RNN)),
            const((None, D_CONV, D_MODEL)), const((None, D_ATT, D_MODEL)),
            const((None, D_RNN, D_MODEL)), const((None, D_MODEL, D_MODEL))]


def _mixer_weights(p):
    return (p['w_sconv'], p['w_rconv'], p['b_rconv'], p['rg_w_a'], p['rg_b_a'], p['rg_w_i'],
            p['rg_b_i'], p['rg_lambda'], p['w_a_out'], p['w_b_out'], p['w_c_out'], p['w_mix_out'])


def mixer_tail_prompt(x, z, os_, ls, p, layer, bsz, s, tm):
    nj = s // tm

    def rows(width, col):
        return pl.BlockSpec((tm, width), lambda b, j: (b * nj + j, col))

    in_specs = ([rows(D_MODEL, 0),
                 rows(D_CONV, 0), rows(D_CONV, 1), rows(D_CONV, 2),
                 rows(D_RNN, OFF_C // D_RNN), rows(D_RNN, OFF_C // D_RNN + 1),
                 rows(D_MODEL, OFF_G // D_MODEL), rows(D_MODEL, OFF_G // D_MODEL + 1),
                 rows(D_MODEL, OFF_G // D_MODEL + 2)]
                + [rows(D_ATT, 0)] * N_GROUPS + [rows(HD, 0)] * N_GROUPS
                + _mixer_weight_specs(layer, 2))
    hdr = SUBLANES
    return pl.pallas_call(
        functools.partial(_mixer_prompt_body, tm=tm),
        grid=(bsz, nj),
        in_specs=in_specs,
        out_specs=[rows(D_MODEL, 0),
                   pl.BlockSpec((None, hdr, D_CONV), lambda b, j: (b, 0, 0)),
                   pl.BlockSpec((None, hdr, D_RNN), lambda b, j: (b, 0, 0))],
        out_shape=[jax.ShapeDtypeStruct((bsz * s, D_MODEL), F32),
                   jax.ShapeDtypeStruct((bsz, hdr, D_CONV), F32),
                   jax.ShapeDtypeStruct((bsz, hdr, D_RNN), F32)],
        scratch_shapes=[pltpu.VMEM((hdr + tm, D_CONV), F32),
                        pltpu.VMEM((hdr + tm, D_RNN), F32),
                        pltpu.VMEM((hdr, D_RNN), F32)],
        compiler_params=_params(2),
        name="mixer_tail_prompt",
    )(x, z, z, z, z, z, z, z, z, *os_, *ls, *_mixer_weights(p))


def mixer_tail_sample(x, z, os_, ls, st_sconv, st_rconv, st_h, p, layer):
    m = x.shape[0]

    def rows(width, col):
        return pl.BlockSpec((m, width), lambda i: (0, col))

    def state(width):
        return pl.BlockSpec((None, m, width), lambda i: (layer, 0, 0))

    in_specs = ([rows(D_MODEL, 0),
                 rows(D_CONV, 0), rows(D_CONV, 1), rows(D_CONV, 2),
                 rows(D_RNN, OFF_C // D_RNN), rows(D_RNN, OFF_C // D_RNN + 1),
                 rows(D_MODEL, OFF_G // D_MODEL), rows(D_MODEL, OFF_G // D_MODEL + 1),
                 rows(D_MODEL, OFF_G // D_MODEL + 2)]
                + [rows(D_ATT, 0)] * N_GROUPS + [rows(HD, 0)] * N_GROUPS
                + [state(2 * D_CONV), state(3 * D_RNN), state(D_RNN)]
                + _mixer_weight_specs(layer, 1))
    return pl.pallas_call(
        _mixer_sample_body,
        grid=(1,),
        in_specs=in_specs,
        out_specs=[rows(D_MODEL, 0), rows(2 * D_CONV, 0), rows(3 * D_RNN, 0), rows(D_RNN, 0)],
        out_shape=[jax.ShapeDtypeStruct((m, D_MODEL), F32),
                   jax.ShapeDtypeStruct((m, 2 * D_CONV), F32),
                   jax.ShapeDtypeStruct((m, 3 * D_RNN), F32),
                   jax.ShapeDtypeStruct((m, D_RNN), F32)],
        compiler_params=_params(1),
        name="mixer_tail_sample",
    )(x, z, z, z, z, z, z, z, z, *os_, *ls, st_sconv, st_rconv, st_h, *_mixer_weights(p))


def _mem_attn_prompt_body(x_ref, g_ref, wq_ref, kv_ref, wo_ref, xo_ref):
    x = x_ref[...]
    hn = _rms(x, g_ref[...]).astype(BF16)
    q = jnp.dot(hn, wq_ref[...], preferred_element_type=F32)
    scale = MEM_HD ** -0.5
    outs = []
    for h in range(MEM_HEADS):
        ks = slice(h * MEM_HD, (h + 1) * MEM_HD)
        vs = slice(D_MODEL + h * MEM_HD, D_MODEL + (h + 1) * MEM_HD)
        sc = lax.dot_general(q[:, ks].astype(BF16), kv_ref[:, ks], NT_DIMS,
                             preferred_element_type=F32) * scale
        e = jnp.exp(sc - jnp.max(sc, axis=-1, keepdims=True))
        pr = (e / jnp.sum(e, axis=-1, keepdims=True)).astype(BF16)
        outs.append(jnp.dot(pr, kv_ref[:, vs], preferred_element_type=F32))
    o = jnp.concatenate(outs, axis=1).astype(BF16)
    xo_ref[...] = x + jnp.dot(o, wo_ref[...], preferred_element_type=F32)


def mem_attention_prompt(x, mem_kv, p, layer, bsz, s, tm):
    nj = s // tm
    rows = pl.BlockSpec((tm, D_MODEL), lambda b, j: (b * nj + j, 0))
    return pl.pallas_call(
        _mem_attn_prompt_body,
        grid=(bsz, nj),
        in_specs=[rows,
                  _resident((None, 1, D_MODEL), lambda b, j: (layer, 0, 0)),
                  _resident((None, D_MODEL, D_MODEL), lambda b, j: (layer, 0, 0)),
                  pl.BlockSpec((None, N_MEM, 2 * D_MODEL), lambda b, j: (b, 0, 0)),
                  _resident((None, D_MODEL, D_MODEL), lambda b, j: (layer, 0, 0))],
        out_specs=rows,
        out_shape=jax.ShapeDtypeStruct((bsz * s, D_MODEL), F32),
        compiler_params=_params(2),
        name="mem_attention_prompt",
    )(x, p['norm_mem'], p['w_mq'], mem_kv, p['w_mo'])


def _mem_attn_sample_body(q_ref, c_ref, o_ref, *, tb):
    scale = MEM_HD ** -0.5
    for b in range(tb):
        for h in range(MEM_HEADS):
            ks = slice(h * MEM_HD, (h + 1) * MEM_HD)
            vs = slice(D_MODEL + h * MEM_HD, D_MODEL + (h + 1) * MEM_HD)
            sc = jnp.sum(c_ref[b, :, ks] * q_ref[b:b + 1, ks], axis=-1, keepdims=True) * scale
            e = jnp.exp(sc - jnp.max(sc, axis=0, keepdims=True))
            pr = e / jnp.sum(e, axis=0, keepdims=True)
            o_ref[b:b + 1, ks] = jnp.sum(pr * c_ref[b, :, vs], axis=0, keepdims=True)


def mem_attention_sample(q, cache, layer):
    m = q.shape[0]
    tb = SUBLANES
    cv = cache.reshape(cache.shape[0], m, N_MEM, 2 * D_MODEL)
    return pl.pallas_call(
        functools.partial(_mem_attn_sample_body, tb=tb),
        grid=(m // tb,),
        in_specs=[pl.BlockSpec((tb, D_MODEL), lambda i: (i, 0)),
                  pl.BlockSpec((None, tb, N_MEM, 2 * D_MODEL), lambda i: (layer, i, 0, 0))],
        out_specs=pl.BlockSpec((tb, D_MODEL), lambda i: (i, 0)),
        out_shape=jax.ShapeDtypeStruct((m, D_MODEL), F32),
        compiler_params=_params(1),
        name="mem_attention_sample",
    )(q, cv)


FF_CHUNK = 512


def _ffn_prompt_body(x_ref, g_ref, wup_ref, wgate_ref, wfc_ref, wdown_ref, xo_ref, fo_ref,
                     hdr_s, uw_s, *, tm):
    hdr = SUBLANES

    @pl.when(pl.program_id(1) == 0)
    def _():
        hdr_s[...] = jnp.zeros((hdr, D_FF), F32)

    x = x_ref[...]
    hn = _rms(x, g_ref[...]).astype(BF16)
    acc = x
    for c in range(D_FF // FF_CHUNK):
        cs = slice(c * FF_CHUNK, (c + 1) * FF_CHUNK)
        u = jnp.dot(hn, wup_ref[:, cs], preferred_element_type=F32)
        gt = jnp.dot(hn, wgate_ref[:, cs], preferred_element_type=F32)
        uw_s[0:hdr, :] = hdr_s[:, cs]
        uw_s[hdr:hdr + tm, :] = u
        wfc = wfc_ref[:, cs]
        uc = (uw_s[hdr - 2:hdr - 2 + tm, :] * wfc[0:1, :]
              + uw_s[hdr - 1:hdr - 1 + tm, :] * wfc[1:2, :]
              + u * wfc[2:3, :])
        hdr_s[:, cs] = u[tm - hdr:tm, :]
        act = (_gelu(uc) * gt).astype(BF16)
        acc = acc + jnp.dot(act, wdown_ref[cs, :], preferred_element_type=F32)
    xo_ref[...] = acc
    fo_ref[...] = hdr_s[...]


def _ffn_sample_body(x_ref, g_ref, st_ref, wup_ref, wgate_ref, wfc_ref, wdown_ref, xo_ref, so_ref):
    x = x_ref[...]
    hn = _rms(x, g_ref[...]).astype(BF16)
    acc = x
    for c in range(D_FF // FF_CHUNK):
        cs = slice(c * FF_CHUNK, (c + 1) * FF_CHUNK)
        cs1 = slice(D_FF + c * FF_CHUNK, D_FF + (c + 1) * FF_CHUNK)
        u = jnp.dot(hn, wup_ref[:, cs], preferred_element_type=F32)
        gt = jnp.dot(hn, wgate_ref[:, cs], preferred_element_type=F32)
        wfc = wfc_ref[:, cs]
        uc = st_ref[:, cs] * wfc[0:1, :] + st_ref[:, cs1] * wfc[1:2, :] + u * wfc[2:3, :]
        so_ref[:, cs] = st_ref[:, cs1]
        so_ref[:, cs1] = u
        act = (_gelu(uc) * gt).astype(BF16)
        acc = acc + jnp.dot(act, wdown_ref[cs, :], preferred_element_type=F32)
    xo_ref[...] = acc


def conv_ffn_prompt(x, p, layer, bsz, s, tm):
    nj = s // tm
    hdr = SUBLANES
    rows = pl.BlockSpec((tm, D_MODEL), lambda b, j: (b * nj + j, 0))
    return pl.pallas_call(
        functools.partial(_ffn_prompt_body, tm=tm),
        grid=(bsz, nj),
        in_specs=[rows,
                  _resident((None, 1, D_MODEL), lambda b, j: (layer, 0, 0)),
                  _resident((None, D_MODEL, D_FF), lambda b, j: (layer, 0, 0)),
                  _resident((None, D_MODEL, D_FF), lambda b, j: (layer, 0, 0)),
                  _resident((None, 3, D_FF), lambda b, j: (layer, 0, 0)),
                  _resident((None, D_FF, D_MODEL), lambda b, j: (layer, 0, 0))],
        out_specs=[rows, pl.BlockSpec((None, hdr, D_FF), lambda b, j: (b, 0, 0))],
        out_shape=[jax.ShapeDtypeStruct((bsz * s, D_MODEL), F32),
                   jax.ShapeDtypeStruct((bsz, hdr, D_FF), F32)],
        scratch_shapes=[pltpu.VMEM((hdr, D_FF), F32), pltpu.VMEM((hdr + tm, FF_CHUNK), F32)],
        compiler_params=_params(2),
        name="conv_ffn_prompt",
    )(x, p['norm_ffn'], p['w_up'], p['w_gate'], p['w_fconv'], p['w_down'])


def conv_ffn_sample(x, st, p, layer):
    m = x.shape[0]
    return pl.pallas_call(
        _ffn_sample_body,
        grid=(1,),
        in_specs=[pl.BlockSpec((m, D_MODEL), lambda i: (0, 0)),
                  pl.BlockSpec((None, 1, D_MODEL), lambda i: (layer, 0, 0)),
                  pl.BlockSpec((None, m, 2 * D_FF), lambda i: (layer, 0, 0)),
                  _resident((None, D_MODEL, D_FF), lambda i: (layer, 0, 0)),
                  _resident((None, D_MODEL, D_FF), lambda i: (layer, 0, 0)),
                  pl.BlockSpec((None, 3, D_FF), lambda i: (layer, 0, 0)),
                  _resident((None, D_FF, D_MODEL), lambda i: (layer, 0, 0))],
        out_specs=[pl.BlockSpec((m, D_MODEL), lambda i: (0, 0)),
                   pl.BlockSpec((m, 2 * D_FF), lambda i: (0, 0))],
        out_shape=[jax.ShapeDtypeStruct((m, D_MODEL), F32),
                   jax.ShapeDtypeStruct((m, 2 * D_FF), F32)],
        compiler_params=_params(1),
        name="conv_ffn_sample",
    )(x, p['norm_ffn'], st, p['w_up'], p['w_gate'], p['w_fconv'], p['w_down'])


def _rope_table(pos):
    half = HD // 2
    inv = ROPE_THETA ** (-jnp.arange(half, dtype=F32) / half)
    ang = pos.astype(F32)[:, None] * inv[None, :]
    cos, sin = jnp.cos(ang), jnp.sin(ang)
    return jnp.concatenate([cos, cos, -sin, sin], axis=-1)


def _prepare_params(norm_mix, w_in, w_sconv, w_a_out, w_b_out, w_rconv, b_rconv, rg_w_a, rg_b_a,
                    rg_w_i, rg_b_i, rg_lambda, w_c_out, w_mix_out, norm_mem, norm_memkv, w_mq,
                    w_mkv, w_mo, norm_ffn, w_up, w_gate, w_fconv, w_down):
    def row(a):
        return a[:, None, :]

    def mx(a):
        return a.astype(BF16)

    return {
        'norm_mix': row(norm_mix), 'w_in': mx(w_in), 'w_sconv': w_sconv, 'w_a_out': mx(w_a_out),
        'w_b_out': mx(w_b_out), 'w_rconv': w_rconv, 'b_rconv': row(b_rconv), 'rg_w_a': mx(rg_w_a),
        'rg_b_a': row(rg_b_a), 'rg_w_i': mx(rg_w_i), 'rg_b_i': row(rg_b_i),
        'rg_lambda': row(rg_lambda), 'w_c_out': mx(w_c_out), 'w_mix_out': mx(w_mix_out),
        'norm_mem': row(norm_mem), 'norm_memkv': row(norm_memkv), 'w_mq': mx(w_mq),
        'w_mkv': mx(w_mkv), 'w_mo': mx(w_mo), 'norm_ffn': row(norm_ffn), 'w_up': mx(w_up),
        'w_gate': mx(w_gate), 'w_fconv': w_fconv, 'w_down': mx(w_down),
    }


def prompt_layer(x, mem, tbl, p, layer, bsz, s, tm_in=1024, tm_mix=256, tm_mem=512, tm_ffn=512):
    mem_kv = rms_matmul(mem, p['norm_memkv'], p['w_mkv'], layer, min(mem.shape[0], 512), 1024)
    z = rms_matmul(x, p['norm_mix'], p['w_in'], layer, tm_in, 1024)
    os_, ls = [], []
    for g, (win, dil) in enumerate(DILATED_GROUPS):
        qc, kc, vc = rope_class_major(z, tbl, g, dil, bsz, s)
        o, lse = band_attention(qc, kc, vc, dil, bsz, s)
        os_.append(o)
        ls.append(lse)
    tail = min(s, max(w for w, _ in DILATED_GROUPS))
    kv_tail = rope_tail(z, tbl, bsz, s, tail).reshape(bsz, tail, N_GROUPS, 2, HG, HD)
    swa = [kv_tail[:, tail - min(win, s):, g] for g, (win, _) in enumerate(DILATED_GROUPS)]
    x, sc8, h8 = mixer_tail_prompt(x, z, os_, ls, p, layer, bsz, s, tm_mix)
    z3 = z.reshape(bsz, s, IN_COLS)
    states = {
        'swa': swa,
        'mem': mem_kv.reshape(bsz, N_MEM, 2, MEM_HEADS, MEM_HD),
        'sconv': sc8[:, SUBLANES - 2:, :],
        'rconv': z3[:, s - 3:, OFF_C:OFF_C + D_RNN],
        'h': h8[:, SUBLANES - 1, :],
    }
    x = mem_attention_prompt(x, mem_kv.reshape(bsz, N_MEM, 2 * D_MODEL).astype(BF16), p, layer,
                             bsz, s, tm_mem)
    x, f8 = conv_ffn_prompt(x, p, layer, bsz, s, tm_ffn)
    states['fconv'] = f8[:, SUBLANES - 2:, :]
    return x, states


def sample_layer(x, caches, cache_mem, st_sconv, st_rconv, st_h, st_ffn, tbl_s, p, layer):
    m = x.shape[0]
    z = rms_matmul(x, p['norm_mix'], p['w_in'], layer, m, 1024)
    os_, ls, kvs = [], [], []
    for g, (win, dil) in enumerate(DILATED_GROUPS):
        o, lse, kv = gather_attention(z, tbl_s, caches[g], layer, g, dil)
        os_.append(o)
        ls.append(lse)
        kvs.append(kv.reshape(m, 1, 2, HG, HD))
    x, sco, rco, ho = mixer_tail_sample(x, z, os_, ls, st_sconv, st_rconv, st_h, p, layer)
    q = rms_matmul(x, p['norm_mem'], p['w_mq'], layer, m, 1024)
    o = mem_attention_sample(q, cache_mem, layer)
    x = proj_residual(x, o, p['w_mo'], layer)
    x, fo = conv_ffn_sample(x, st_ffn, p, layer)
    states = {'swa': kvs, 'sconv': sco.reshape(m, 2, D_CONV), 'rconv': rco.reshape(m, 3, D_RNN),
              'h': ho, 'fconv': fo.reshape(m, 2, D_FF)}
    return x, states


def kernel(x_prompt, x_sample, cache_swa1_kv, cache_swa2_kv, cache_swa3_kv, cache_mem_kv, state_sconv, state_rglru_conv, state_rglru_h, state_ffn_conv, mem_prompt, norm_mix, w_in, w_sconv, w_a_out, w_b_out, w_rconv, b_rconv, rg_w_a, rg_b_a, rg_w_i, rg_b_i, rg_lambda, w_c_out, w_mix_out, norm_mem, norm_memkv, w_mq, w_mkv, w_mo, norm_ffn, w_up, w_gate, w_fconv, w_down, norm_final):
    p = _prepare_params(norm_mix, w_in, w_sconv, w_a_out, w_b_out, w_rconv, b_rconv, rg_w_a,
                        rg_b_a, rg_w_i, rg_b_i, rg_lambda, w_c_out, w_mix_out, norm_mem,
                        norm_memkv, w_mq, w_mkv, w_mo, norm_ffn, w_up, w_gate, w_fconv, w_down)
    bsz, s, _ = x_prompt.shape
    m, ts, _ = x_sample.shape
    assert ts == 1, "the sample group advances one token per step"
    depth = w_in.shape[0]
    tbl_p = _rope_table(jnp.arange(s))
    tbl_s = _rope_table(PAST_LEN + jnp.arange(ts))
    caches = (cache_swa1_kv, cache_swa2_kv, cache_swa3_kv)
    st_sconv = state_sconv.reshape(depth, m, 2 * D_CONV)
    st_rconv = state_rglru_conv.reshape(depth, m, 3 * D_RNN)
    st_ffn = state_ffn_conv.reshape(depth, m, 2 * D_FF)

    xp = x_prompt.reshape(bsz * s, D_MODEL)
    xs = x_sample.reshape(m, D_MODEL)
    mem = mem_prompt.reshape(bsz * N_MEM, D_MODEL)
    ps, ss = [], []
    for layer in range(depth):
        xp, st = prompt_layer(xp, mem, tbl_p, p, layer, bsz, s)
        ps.append(st)
        xs, st = sample_layer(xs, caches, cache_mem_kv, st_sconv, st_rconv, state_rglru_h, st_ffn,
                              tbl_s, p, layer)
        ss.append(st)
    gf = norm_final[None, :]
    y_prompt = rmsnorm_rows(xp, gf, 1024).reshape(bsz, s, D_MODEL)
    y_sample = rmsnorm_rows(xs, gf, m).reshape(m, ts, D_MODEL)

    def stk(sts, key):
        return jnp.stack([st[key] for st in sts])

    def stk_swa(sts, g):
        return jnp.stack([st['swa'][g] for st in sts])

    return (y_prompt, y_sample,
            stk_swa(ps, 0), stk_swa(ps, 1), stk_swa(ps, 2),
            stk(ps, 'mem'), stk(ps, 'sconv'), stk(ps, 'rconv'), stk(ps, 'h'), stk(ps, 'fconv'),
            stk_swa(ss, 0), stk_swa(ss, 1), stk_swa(ss, 2),
            stk(ss, 'sconv'), stk(ss, 'rconv'), stk(ss, 'h'), stk(ss, 'fconv'))
```
